```python
import math
import jax, jax.numpy as jnp
from jax import lax
import numpy as np

D_MODEL = 1024
BATCH = 16
SEQ = 4096
DEPTH = 1

POOL_GROUPS = 4
POOL_WINDOWS = (2, 4, 8, 16)
POOL_WIDTH = D_MODEL // 2
POOL_GROUP_DIM = POOL_WIDTH // POOL_GROUPS
N_HEADS = 8
QK_NOPE_DIM = 128
QK_ROPE_DIM = 64
V_DIM = 128
Q_LORA = D_MODEL // 4
KV_LORA = D_MODEL // 8
QK_DIM = QK_NOPE_DIM + QK_ROPE_DIM
ROPE_THETA = 10000.0
Q_BLOCK = 128
D_FF = 4 * D_MODEL
EPS = 1e-6
IN_COLS = POOL_WIDTH + Q_LORA + KV_LORA + QK_ROPE_DIM + 2 * D_MODEL
SPLITS = tuple(np.cumsum([POOL_WIDTH, Q_LORA, KV_LORA, QK_ROPE_DIM, D_MODEL]).tolist())

kernel_name = "hybrid_pool_mla_gated_block"


def rmsnorm(x, g):
    xf = x.astype(jnp.float32)
    xf = xf * lax.rsqrt(jnp.mean(xf * xf, axis=-1, keepdims=True) + EPS)
    return (xf * g.astype(jnp.float32)).astype(x.dtype)


def rope(x, cos, sin):
    x1, x2 = jnp.split(x, 2, axis=-1)
    return jnp.concatenate([x1 * cos - x2 * sin, x2 * cos + x1 * sin], axis=-1)


def causal_window_mean(u, w):
    s = u.shape[1]
    cs = jnp.cumsum(u, axis=1)
    lag = jnp.pad(cs[:, : s - w], ((0, 0), (w, 0), (0, 0)))
    count = jnp.minimum(jnp.arange(1, s + 1), w).astype(jnp.float32)
    return (cs - lag) / count[None, :, None]


def pool_mixer(u, pool_w, pool_scale, w_proj):
    b, s, _ = u.shape
    ug = u.reshape(b, s, POOL_GROUPS, POOL_GROUP_DIM)
    pooled = []
    for g, w in enumerate(POOL_WINDOWS):
        ui = ug[:, :, g].astype(jnp.float32)
        pooled.append(causal_window_mean(ui, w) - ui)
    pooled = jnp.stack(pooled, axis=2).astype(u.dtype)
    mixed = jnp.einsum('bsgc,gcd->bsgd', pooled, pool_w).reshape(b, s, POOL_WIDTH)
    return (mixed * pool_scale) @ w_proj


def mla_mixer(c_q, c_kv, k_rope_in, cos, sin, g_q, w_uq, g_kv, w_ukv, w_proj):
    b, s, _ = c_q.shape
    q = (rmsnorm(c_q, g_q) @ w_uq).reshape(b, s, N_HEADS, QK_DIM)
    q_nope, q_rope = q[..., :QK_NOPE_DIM], q[..., QK_NOPE_DIM:]
    q_rope = rope(q_rope, cos[:, :, None, :], sin[:, :, None, :])
    kv = (rmsnorm(c_kv, g_kv) @ w_ukv).reshape(b, s, N_HEADS, QK_NOPE_DIM + V_DIM)
    k_nope, v = kv[..., :QK_NOPE_DIM], kv[..., QK_NOPE_DIM:]
    k_rope = rope(k_rope_in, cos, sin)
    scale = 1.0 / math.sqrt(QK_DIM)
    k_idx = jnp.arange(s)

    def block(i):
        start = i * Q_BLOCK
        qn = lax.dynamic_slice_in_dim(q_nope, start, Q_BLOCK, axis=1)
        qr = lax.dynamic_slice_in_dim(q_rope, start, Q_BLOCK, axis=1)
        sc = (jnp.einsum('bqhd,bkhd->bhqk', qn, k_nope)
              + jnp.einsum('bqhd,bkd->bhqk', qr, k_rope)).astype(jnp.float32) * scale
        q_idx = start + jnp.arange(Q_BLOCK)
        mask = k_idx[None, :] <= q_idx[:, None]
        p = jax.nn.softmax(jnp.where(mask, sc, -jnp.inf), axis=-1).astype(v.dtype)
        return jnp.einsum('bhqk,bkhd->bqhd', p, v)

    o = lax.map(block, jnp.arange(s // Q_BLOCK))
    o = jnp.transpose(o, (1, 0, 2, 3, 4)).reshape(b, s, N_HEADS * V_DIM)
    return o @ w_proj


def setup_inputs(seed: int = 0) -> dict:
    key = jax.random.key(seed)
    ks = jax.random.split(key, 24)
    L, D = DEPTH, D_MODEL

    def w(k, shape, fan_in):
        return jax.random.normal(k, shape, jnp.float32) * fan_in ** -0.5

    def gain(k, shape):
        return 1.0 + 0.01 * jax.random.normal(k, shape, jnp.float32)

    x = jax.random.normal(ks[0], (BATCH, SEQ, D), jnp.float32)
    offsets = jax.random.randint(ks[1], (BATCH, 1), 0, 1024, jnp.int32)
    positions = (jnp.arange(SEQ, dtype=jnp.int32)[None, :] + offsets).astype(jnp.int32)
    return {
        "x": x,
        "positions": positions,
        "g_mix": gain(ks[2], (L, D)),
        "w_in": w(ks[3], (L, D, IN_COLS), D),
        "b_gate": 0.01 * jax.random.normal(ks[4], (L, 2 * D), jnp.float32),
        "pool_w": w(ks[5], (L, POOL_GROUPS, POOL_GROUP_DIM, POOL_GROUP_DIM), POOL_GROUP_DIM),
        "pool_scale": gain(ks[6], (L, POOL_WIDTH)),
        "w_pool_proj": w(ks[7], (L, POOL_WIDTH, D), POOL_WIDTH),
        "g_q": gain(ks[8], (L, Q_LORA)),
        "w_uq": w(ks[9], (L, Q_LORA, N_HEADS * QK_DIM), Q_LORA),
        "g_kv": gain(ks[10], (L, KV_LORA)),
        "w_ukv": w(ks[11], (L, KV_LORA, N_HEADS * (QK_NOPE_DIM + V_DIM)), KV_LORA),
        "w_attn_proj": w(ks[12], (L, N_HEADS * V_DIM, D), N_HEADS * V_DIM),
        "w_out": w(ks[13], (L, D, D), D),
        "g_mlp": gain(ks[14], (L, D)),
        "w_mlp_in": w(ks[15], (L, D, D_FF), D),
        "w_mlp_out": w(ks[16], (L, D_FF, D), D_FF),
        "g_final": gain(ks[17], (D,)),
    }


def reference(x, positions, g_mix, w_in, b_gate, pool_w, pool_scale, w_pool_proj,
              g_q, w_uq, g_kv, w_ukv, w_attn_proj, w_out, g_mlp, w_mlp_in,
              w_mlp_out, g_final):
    half = QK_ROPE_DIM // 2
    inv_freq = ROPE_THETA ** (-jnp.arange(half, dtype=jnp.float32) / half)
    ang = positions.astype(jnp.float32)[..., None] * inv_freq
    cos = jnp.cos(ang).astype(x.dtype)
    sin = jnp.sin(ang).astype(x.dtype)

    for l in range(DEPTH):
        h = rmsnorm(x, g_mix[l])
        z = h @ w_in[l]
        u_pool, c_q, c_kv, k_rope_in, gate_a, gate_b = jnp.split(z, SPLITS, axis=-1)
        gate_a = jax.nn.sigmoid(gate_a + b_gate[l, :D_MODEL])
        gate_b = jax.nn.sigmoid(gate_b + b_gate[l, D_MODEL:])
        a = pool_mixer(u_pool, pool_w[l], pool_scale[l], w_pool_proj[l])
        bm = mla_mixer(c_q, c_kv, k_rope_in, cos, sin, g_q[l], w_uq[l], g_kv[l],
                       w_ukv[l], w_attn_proj[l])
        x = x + (gate_a * a + gate_b * bm) @ w_out[l]
        hm = rmsnorm(x, g_mlp[l])
        x = x + jnp.square(jax.nn.relu(hm @ w_mlp_in[l])) @ w_mlp_out[l]

    return rmsnorm(x, g_final)
```

```python
import functools
import math

import jax
import jax.numpy as jnp
from jax import lax
from jax.experimental import pallas as pl
from jax.experimental.pallas import tpu as pltpu

D_MODEL = 1024
POOL_GROUPS = 4
POOL_WINDOWS = (2, 4, 8, 16)
POOL_WIDTH = D_MODEL // 2
POOL_GROUP_DIM = POOL_WIDTH // POOL_GROUPS
N_HEADS = 8
QK_NOPE_DIM = 128
QK_ROPE_DIM = 64
ROPE_HALF = QK_ROPE_DIM // 2
V_DIM = 128
Q_LORA = D_MODEL // 4
KV_LORA = D_MODEL // 8
QK_DIM = QK_NOPE_DIM + QK_ROPE_DIM
ROPE_THETA = 10000.0
D_FF = 4 * D_MODEL
EPS = 1e-6

POOL_HALO = 16
V7X_VMEM_LIMIT_BYTES = 56 * 1024 * 1024

PROJ_TOKENS = 512
ATTN_BLOCK = 512
MLP_TOKENS = 512
FF_CHUNK = 1024

BF16 = jnp.bfloat16
F32 = jnp.float32


def _dot(a, b):
    return jnp.dot(a, b, preferred_element_type=F32)


def _dot_nt(a, b):
    return lax.dot_general(a, b, (((1,), (1,)), ((), ())), preferred_element_type=F32)


def _rms_scale(x):
    return lax.rsqrt(jnp.mean(x * x, axis=-1, keepdims=True) + EPS)


def _const_spec(shape):
    zeros = (0,) * len(shape)
    return pl.BlockSpec(shape, lambda *_: zeros, pipeline_mode=pl.Buffered(1))


def _proj_kernel(x_ref, pos_ref, invf_ref, gmix_ref, wa_ref, wg_ref, bg_ref, poolw_ref,
                 pscale_ref, wpp_ref, gq_ref, wuqt_ref, gkv_ref, wuk_ref, wuvt_ref,
                 ga_ref, gb_ref, qt_ref, kn_ref, kr_ref, vt_ref, ubuf_ref):
    tm = PROJ_TOKENS
    j = pl.program_id(1)

    x = x_ref[0]
    h = (x * _rms_scale(x) * gmix_ref[...]).astype(BF16)
    za = _dot(h, wa_ref[...])
    zg = _dot(h, wg_ref[...]) + bg_ref[...]

    u = za[:, :POOL_WIDTH]

    @pl.when(j == 0)
    def _():
        ubuf_ref[0:POOL_HALO, :] = jnp.zeros((POOL_HALO, POOL_WIDTH), F32)

    ubuf_ref[POOL_HALO:POOL_HALO + tm, :] = u
    t_in_seq = j * tm + lax.broadcasted_iota(jnp.int32, (tm, POOL_GROUP_DIM), 0)
    mixed = []
    for g, w in enumerate(POOL_WINDOWS):
        c0 = g * POOL_GROUP_DIM
        ug = u[:, c0:c0 + POOL_GROUP_DIM]
        win = ug
        for k in range(1, w):
            win = win + ubuf_ref[POOL_HALO - k:POOL_HALO - k + tm, c0:c0 + POOL_GROUP_DIM]
        count = jnp.minimum(t_in_seq + 1, w).astype(F32)
        pooled = (win / count - ug).astype(BF16)
        mixed.append(_dot(pooled, poolw_ref[g]))
    ubuf_ref[0:POOL_HALO, :] = u[tm - POOL_HALO:, :]
    mixed = jnp.concatenate(mixed, axis=-1) * pscale_ref[...]
    a = _dot(mixed.astype(BF16), wpp_ref[...])

    ga_ref[0] = (jax.nn.sigmoid(zg[:, :D_MODEL]) * a).astype(BF16)
    gb_ref[0] = jax.nn.sigmoid(zg[:, D_MODEL:]).astype(BF16)

    ang_t = invf_ref[...] * pos_ref[0].astype(F32)
    cos_t = jnp.cos(ang_t)
    sin_t = jnp.sin(ang_t)
    cos = cos_t.T
    sin = sin_t.T

    c_q = za[:, POOL_WIDTH:POOL_WIDTH + Q_LORA]
    q_scale = 1.0 / math.sqrt(QK_DIM)
    cqn = (c_q * _rms_scale(c_q) * (gq_ref[...] * q_scale)).astype(BF16)
    qt = _dot_nt(wuqt_ref[...], cqn)
    for hd in range(N_HEADS):
        r0 = hd * QK_DIM
        r1 = r0 + QK_NOPE_DIM
        qt_ref[0, r0:r1, :] = qt[r0:r1].astype(BF16)
        x1 = qt[r1:r1 + ROPE_HALF]
        x2 = qt[r1 + ROPE_HALF:r1 + QK_ROPE_DIM]
        qt_ref[0, r1:r1 + ROPE_HALF, :] = (x1 * cos_t - x2 * sin_t).astype(BF16)
        qt_ref[0, r1 + ROPE_HALF:r1 + QK_ROPE_DIM, :] = (x2 * cos_t + x1 * sin_t).astype(BF16)

    kv0 = POOL_WIDTH + Q_LORA
    c_kv = za[:, kv0:kv0 + KV_LORA]
    ckvn = (c_kv * _rms_scale(c_kv) * gkv_ref[...]).astype(BF16)
    kn_ref[0] = _dot(ckvn, wuk_ref[...]).astype(BF16)
    vt_ref[0] = _dot_nt(wuvt_ref[...], ckvn).astype(BF16)

    kr0 = kv0 + KV_LORA
    k1 = za[:, kr0:kr0 + ROPE_HALF]
    k2 = za[:, kr0 + ROPE_HALF:kr0 + QK_ROPE_DIM]
    kr_ref[0] = jnp.concatenate([k1 * cos - k2 * sin, k2 * cos + k1 * sin], axis=-1).astype(BF16)


def _projections(x, pos_row, invf, gmix, wa, wg, bg, poolw, pscale, wpp, gq, wuqt, gkv, wuk, wuvt):
    b, s, d = x.shape
    tm = PROJ_TOKENS
    nt = s // tm
    tok = lambda bi, j: (bi, j, 0)
    feat = lambda bi, j: (bi, 0, j)
    out_shape = (
        jax.ShapeDtypeStruct((b, s, d), BF16),
        jax.ShapeDtypeStruct((b, s, d), BF16),
        jax.ShapeDtypeStruct((b, N_HEADS * QK_DIM, s), BF16),
        jax.ShapeDtypeStruct((b, s, N_HEADS * QK_NOPE_DIM), BF16),
        jax.ShapeDtypeStruct((b, s, QK_ROPE_DIM), BF16),
        jax.ShapeDtypeStruct((b, N_HEADS * V_DIM, s), BF16),
    )
    return pl.pallas_call(
        _proj_kernel,
        grid=(b, nt),
        in_specs=[
            pl.BlockSpec((1, tm, d), tok),
            pl.BlockSpec((1, 1, tm), feat),
            _const_spec(invf.shape), _const_spec(gmix.shape), _const_spec(wa.shape),
            _const_spec(wg.shape), _const_spec(bg.shape), _const_spec(poolw.shape),
            _const_spec(pscale.shape), _const_spec(wpp.shape), _const_spec(gq.shape),
            _const_spec(wuqt.shape), _const_spec(gkv.shape), _const_spec(wuk.shape),
            _const_spec(wuvt.shape),
        ],
        out_specs=(
            pl.BlockSpec((1, tm, d), tok),
            pl.BlockSpec((1, tm, d), tok),
            pl.BlockSpec((1, N_HEADS * QK_DIM, tm), feat),
            pl.BlockSpec((1, tm, N_HEADS * QK_NOPE_DIM), tok),
            pl.BlockSpec((1, tm, QK_ROPE_DIM), tok),
            pl.BlockSpec((1, N_HEADS * V_DIM, tm), feat),
        ),
        out_shape=out_shape,
        scratch_shapes=[pltpu.VMEM((POOL_HALO + tm, POOL_WIDTH), F32)],
        compiler_params=pltpu.CompilerParams(
            dimension_semantics=("arbitrary", "arbitrary"),
            vmem_limit_bytes=V7X_VMEM_LIMIT_BYTES),
        name="mla_pool_projections",
    )(x, pos_row, invf, gmix, wa, wg, bg, poolw, pscale, wpp, gq, wuqt, gkv, wuk, wuvt)


def _attn_kernel(qt_ref, kn_ref, kr_ref, vt_ref, o_ref):
    blk = ATTN_BLOCK
    i = pl.program_id(2)
    qt = qt_ref[0]

    def step(jb, carry, masked):
        m, l, acc = carry
        ks = pl.multiple_of(jb * blk, blk)
        k = jnp.concatenate([kn_ref[0, pl.ds(ks, blk), :], kr_ref[0, pl.ds(ks, blk), :]], axis=-1)
        s = _dot(k, qt)
        if masked:
            kidx = lax.broadcasted_iota(jnp.int32, (blk, blk), 0)
            qidx = lax.broadcasted_iota(jnp.int32, (blk, blk), 1)
            s = jnp.where(kidx <= qidx, s, -jnp.inf)
        m_new = jnp.maximum(m, jnp.max(s, axis=0, keepdims=True))
        p = jnp.exp(s - m_new)
        alpha = jnp.exp(m - m_new)
        l = alpha * l + jnp.sum(p, axis=0, keepdims=True)
        v = vt_ref[0, :, pl.ds(ks, blk)]
        acc = alpha * acc + _dot(v, p.astype(BF16))
        return m_new, l, acc

    init = (jnp.full((1, blk), -jnp.inf, F32), jnp.zeros((1, blk), F32),
            jnp.zeros((V_DIM, blk), F32))
    carry = lax.fori_loop(0, i, lambda jb, c: step(jb, c, False), init)
    _, l, acc = step(i, carry, True)
    o_ref[0] = (acc / l).T.astype(BF16)


def _attention(qt, kn, kr, vt):
    b = qt.shape[0]
    s = qt.shape[2]
    blk = ATTN_BLOCK
    return pl.pallas_call(
        _attn_kernel,
        grid=(b, N_HEADS, s // blk),
        in_specs=[
            pl.BlockSpec((1, QK_DIM, blk), lambda bi, h, i: (bi, h, i)),
            pl.BlockSpec((1, s, QK_NOPE_DIM), lambda bi, h, i: (bi, 0, h)),
            pl.BlockSpec((1, s, QK_ROPE_DIM), lambda bi, h, i: (bi, 0, 0)),
            pl.BlockSpec((1, V_DIM, s), lambda bi, h, i: (bi, h, 0)),
        ],
        out_specs=pl.BlockSpec((1, blk, V_DIM), lambda bi, h, i: (bi, i, h)),
        out_shape=jax.ShapeDtypeStruct((b, s, N_HEADS * V_DIM), BF16),
        compiler_params=pltpu.CompilerParams(
            dimension_semantics=("arbitrary", "arbitrary", "arbitrary"),
            vmem_limit_bytes=V7X_VMEM_LIMIT_BYTES),
        name="mla_causal_attention",
    )(qt, kn, kr, vt)


def _out_kernel(x_ref, o_ref, ga_ref, gb_ref, wap_ref, wout_ref, gmlp_ref, w1_ref, w2_ref,
                gfin_ref, out_ref):
    bm = _dot(o_ref[...], wap_ref[...])
    y = ga_ref[...].astype(F32) + gb_ref[...].astype(F32) * bm
    x1 = x_ref[...] + _dot(y.astype(BF16), wout_ref[...])
    hm = (x1 * _rms_scale(x1) * gmlp_ref[...]).astype(BF16)
    x2 = x1
    for c in range(D_FF // FF_CHUNK):
        c0 = c * FF_CHUNK
        hid = jnp.square(jnp.maximum(_dot(hm, w1_ref[:, c0:c0 + FF_CHUNK]), 0.0))
        x2 = x2 + _dot(hid.astype(BF16), w2_ref[c0:c0 + FF_CHUNK, :])
    out_ref[...] = x2 * _rms_scale(x2) * gfin_ref[...]


def _output(x2d, o2d, ga2d, gb2d, wap, wout, gmlp, w1, w2, gfin):
    t, d = x2d.shape
    tm = MLP_TOKENS
    row = lambda i: (i, 0)
    return pl.pallas_call(
        _out_kernel,
        grid=(t // tm,),
        in_specs=[
            pl.BlockSpec((tm, d), row), pl.BlockSpec((tm, d), row),
            pl.BlockSpec((tm, d), row), pl.BlockSpec((tm, d), row),
            _const_spec(wap.shape), _const_spec(wout.shape), _const_spec(gmlp.shape),
            _const_spec(w1.shape), _const_spec(w2.shape), _const_spec(gfin.shape),
        ],
        out_specs=pl.BlockSpec((tm, d), row),
        out_shape=jax.ShapeDtypeStruct((t, d), F32),
        compiler_params=pltpu.CompilerParams(
            dimension_semantics=("arbitrary",),
            vmem_limit_bytes=V7X_VMEM_LIMIT_BYTES),
        name="gated_merge_mlp",
    )(x2d, o2d, ga2d, gb2d, wap, wout, gmlp, w1, w2, gfin)


def kernel(x, positions, g_mix, w_in, b_gate, pool_w, pool_scale, w_pool_proj, g_q, w_uq, g_kv,
           w_ukv, w_attn_proj, w_out, g_mlp, w_mlp_in, w_mlp_out, g_final):
    b, s, d = x.shape
    assert d == D_MODEL and s % PROJ_TOKENS == 0 and s % ATTN_BLOCK == 0
    assert (b * s) % MLP_TOKENS == 0 and g_mix.shape[0] == 1
    l = 0

    front = POOL_WIDTH + Q_LORA + KV_LORA + QK_ROPE_DIM
    wa = jnp.pad(w_in[l][:, :front], ((0, 0), (0, D_MODEL - front))).astype(BF16)
    wg = w_in[l][:, front:].astype(BF16)
    bg = b_gate[l].reshape(1, 2 * D_MODEL)
    wuqt = w_uq[l].T.astype(BF16)
    wukv = w_ukv[l].reshape(KV_LORA, N_HEADS, QK_NOPE_DIM + V_DIM)
    wuk = wukv[:, :, :QK_NOPE_DIM].reshape(KV_LORA, N_HEADS * QK_NOPE_DIM).astype(BF16)
    wuvt = wukv[:, :, QK_NOPE_DIM:].reshape(KV_LORA, N_HEADS * V_DIM).T.astype(BF16)
    half = ROPE_HALF
    invf = (ROPE_THETA ** (-jnp.arange(half, dtype=F32) / half)).reshape(half, 1)

    ga, gb, qt, kn, kr, vt = _projections(
        x, positions.reshape(b, 1, s), invf, g_mix[l].reshape(1, d), wa, wg, bg,
        pool_w[l].astype(BF16), pool_scale[l].reshape(1, POOL_WIDTH),
        w_pool_proj[l].astype(BF16), g_q[l].reshape(1, Q_LORA), wuqt,
        g_kv[l].reshape(1, KV_LORA), wuk, wuvt)

    o = _attention(qt, kn, kr, vt)

    t = b * s
    out = _output(
        x.reshape(t, d), o.reshape(t, d), ga.reshape(t, d), gb.reshape(t, d),
        w_attn_proj[l].astype(BF16), w_out[l].astype(BF16), g_mlp[l].reshape(1, d),
        w_mlp_in[l].astype(BF16), w_mlp_out[l].astype(BF16), g_final.reshape(1, d))
    return out.reshape(b, s, d)
```

```python
import functools
import math

import jax
import jax.numpy as jnp
from jax import lax
from jax.experimental import pallas as pl
from jax.experimental.pallas import tpu as pltpu

D_MODEL = 1024
POOL_GROUPS = 4
POOL_WINDOWS = (2, 4, 8, 16)
POOL_WIDTH = D_MODEL // 2
POOL_GROUP_DIM = POOL_WIDTH // POOL_GROUPS
N_HEADS = 8
QK_NOPE_DIM = 128
QK_ROPE_DIM = 64
ROPE_HALF = QK_ROPE_DIM // 2
V_DIM = 128
Q_LORA = D_MODEL // 4
KV_LORA = D_MODEL // 8
QK_DIM = QK_NOPE_DIM + QK_ROPE_DIM
ROPE_THETA = 10000.0
D_FF = 4 * D_MODEL
EPS = 1e-6

POOL_HALO = 16
BF16_SUBLANES = 16
V7X_VMEM_LIMIT_BYTES = 56 * 1024 * 1024

PROJ_TOKENS = 512
ATTN_BLOCK = 512
MLP_TOKENS = 512
FF_CHUNK = 1024

BF16 = jnp.bfloat16
F32 = jnp.float32


def _dot(a, b):
    return jnp.dot(a, b, preferred_element_type=F32)


def _dot_nt(a, b):
    return lax.dot_general(a, b, (((1,), (1,)), ((), ())), preferred_element_type=F32)


def _rms_scale(x):
    return lax.rsqrt(jnp.mean(x * x, axis=-1, keepdims=True) + EPS)


def _const_spec(shape):
    zeros = (0,) * len(shape)
    return pl.BlockSpec(shape, lambda *_: zeros, pipeline_mode=pl.Buffered(1))


def _proj_kernel(x_ref, pos_ref, invf_ref, gmix_ref, wa_ref, wg_ref, bg_ref, poolw_ref,
                 pscale_ref, wpp_ref, gq_ref, wuqt_ref, gkv_ref, wuk_ref, wuvt_ref,
                 ga_ref, gb_ref, qt_ref, kn_ref, kr_ref, vt_ref, ubuf_ref):
    tm = PROJ_TOKENS
    j = pl.program_id(1)

    x = x_ref[0]
    h = (x * _rms_scale(x) * gmix_ref[...]).astype(BF16)
    za = _dot(h, wa_ref[...])
    zg = _dot(h, wg_ref[...]) + bg_ref[...]

    u = za[:, :POOL_WIDTH]

    @pl.when(j == 0)
    def _():
        ubuf_ref[0:POOL_HALO, :] = jnp.zeros((POOL_HALO, POOL_WIDTH), F32)

    ubuf_ref[POOL_HALO:POOL_HALO + tm, :] = u
    t_in_seq = j * tm + lax.broadcasted_iota(jnp.int32, (tm, POOL_GROUP_DIM), 0)
    mixed = []
    for g, w in enumerate(POOL_WINDOWS):
        c0 = g * POOL_GROUP_DIM
        ug = u[:, c0:c0 + POOL_GROUP_DIM]
        win = ug
        for k in range(1, w):
            win = win + ubuf_ref[POOL_HALO - k:POOL_HALO - k + tm, c0:c0 + POOL_GROUP_DIM]
        count = jnp.minimum(t_in_seq + 1, w).astype(F32)
        pooled = (win / count - ug).astype(BF16)
        mixed.append(_dot(pooled, poolw_ref[g]))
    ubuf_ref[0:POOL_HALO, :] = u[tm - POOL_HALO:, :]
    mixed = jnp.concatenate(mixed, axis=-1) * pscale_ref[...]
    a = _dot(mixed.astype(BF16), wpp_ref[...])

    ga_ref[0] = (jax.nn.sigmoid(zg[:, :D_MODEL]) * a).astype(BF16)
    gb_ref[0] = jax.nn.sigmoid(zg[:, D_MODEL:]).astype(BF16)

    ang_t = invf_ref[...] * pos_ref[0].astype(F32)
    cos_t = jnp.cos(ang_t)
    sin_t = jnp.sin(ang_t)
    cos = cos_t.T
    sin = sin_t.T

    c_q = za[:, POOL_WIDTH:POOL_WIDTH + Q_LORA]
    q_scale = math.log2(math.e) / math.sqrt(QK_DIM)
    cqn = (c_q * _rms_scale(c_q) * (gq_ref[...] * q_scale)).astype(BF16)
    qt = _dot_nt(wuqt_ref[...], cqn)
    for hd in range(N_HEADS):
        r0 = hd * QK_DIM
        r1 = r0 + QK_NOPE_DIM
        qt_ref[0, r0:r1, :] = qt[r0:r1].astype(BF16)
        x1 = qt[r1:r1 + ROPE_HALF]
        x2 = qt[r1 + ROPE_HALF:r1 + QK_ROPE_DIM]
        qt_ref[0, r1:r1 + ROPE_HALF, :] = (x1 * cos_t - x2 * sin_t).astype(BF16)
        qt_ref[0, r1 + ROPE_HALF:r1 + QK_ROPE_DIM, :] = (x2 * cos_t + x1 * sin_t).astype(BF16)

    kv0 = POOL_WIDTH + Q_LORA
    c_kv = za[:, kv0:kv0 + KV_LORA]
    ckvn = (c_kv * _rms_scale(c_kv) * gkv_ref[...]).astype(BF16)
    kn_ref[0] = _dot(ckvn, wuk_ref[...]).astype(BF16)
    vt_ref[0] = _dot_nt(wuvt_ref[...], ckvn).astype(BF16)

    kr0 = kv0 + KV_LORA
    k1 = za[:, kr0:kr0 + ROPE_HALF]
    k2 = za[:, kr0 + ROPE_HALF:kr0 + QK_ROPE_DIM]
    kr_ref[0] = jnp.concatenate([k1 * cos - k2 * sin, k2 * cos + k1 * sin], axis=-1).astype(BF16)


def _projections(x, pos_row, invf, gmix, wa, wg, bg, poolw, pscale, wpp, gq, wuqt, gkv, wuk, wuvt):
    b, s, d = x.shape
    tm = PROJ_TOKENS
    nt = s // tm
    tok = lambda bi, j: (bi, j, 0)
    feat = lambda bi, j: (bi, 0, j)
    out_shape = (
        jax.ShapeDtypeStruct((b, s, d), BF16),
        jax.ShapeDtypeStruct((b, s, d), BF16),
        jax.ShapeDtypeStruct((b, N_HEADS * QK_DIM, s), BF16),
        jax.ShapeDtypeStruct((b, s, N_HEADS * QK_NOPE_DIM), BF16),
        jax.ShapeDtypeStruct((b, s, QK_ROPE_DIM), BF16),
        jax.ShapeDtypeStruct((b, N_HEADS * V_DIM, s), BF16),
    )
    return pl.pallas_call(
        _proj_kernel,
        grid=(b, nt),
        in_specs=[
            pl.BlockSpec((1, tm, d), tok),
            pl.BlockSpec((1, 1, tm), feat),
            _const_spec(invf.shape), _const_spec(gmix.shape), _const_spec(wa.shape),
            _const_spec(wg.shape), _const_spec(bg.shape), _const_spec(poolw.shape),
            _const_spec(pscale.shape), _const_spec(wpp.shape), _const_spec(gq.shape),
            _const_spec(wuqt.shape), _const_spec(gkv.shape), _const_spec(wuk.shape),
            _const_spec(wuvt.shape),
        ],
        out_specs=(
            pl.BlockSpec((1, tm, d), tok),
            pl.BlockSpec((1, tm, d), tok),
            pl.BlockSpec((1, N_HEADS * QK_DIM, tm), feat),
            pl.BlockSpec((1, tm, N_HEADS * QK_NOPE_DIM), tok),
            pl.BlockSpec((1, tm, QK_ROPE_DIM), tok),
            pl.BlockSpec((1, N_HEADS * V_DIM, tm), feat),
        ),
        out_shape=out_shape,
        scratch_shapes=[pltpu.VMEM((POOL_HALO + tm, POOL_WIDTH), F32)],
        compiler_params=pltpu.CompilerParams(
            dimension_semantics=("arbitrary", "arbitrary"),
            vmem_limit_bytes=V7X_VMEM_LIMIT_BYTES),
        name="mla_pool_projections",
    )(x, pos_row, invf, gmix, wa, wg, bg, poolw, pscale, wpp, gq, wuqt, gkv, wuk, wuvt)


def _attn_steps(nq):
    return [(i, i) for i in range(nq)] + [(i, j) for j in range(nq) for i in range(j + 1, nq)]


def _attn_kernel(steps_ref, qt_ref, kn_ref, kr_ref, vt_ref, o_ref,
                 s_buf, cm_buf, p_buf, al_buf, m_ref, acc_ref):
    blk = ATTN_BLOCK
    nq = qt_ref.shape[2] // blk
    n_couples = nq * (nq + 1) // 4
    n_diag_couples = nq // 2
    ones = jnp.ones((BF16_SUBLANES, blk), BF16)

    m_ref[...] = jnp.full(m_ref.shape, -jnp.inf, F32)
    acc_ref[...] = jnp.zeros(acc_ref.shape, F32)

    def off(b):
        return pl.multiple_of(b * blk, blk)

    def qk(w, slot, masked=False):
        for e in range(2):
            u = 2 * w + e
            i, j = steps_ref[0, u], steps_ref[1, u]
            k = jnp.concatenate(
                [kn_ref[0, pl.ds(off(j), blk), :], kr_ref[0, pl.ds(off(j), blk), :]], axis=-1)
            s = _dot(k, qt_ref[0, :, pl.ds(off(i), blk)])
            if masked:
                causal = (lax.broadcasted_iota(jnp.int32, (blk, blk), 0)
                          <= lax.broadcasted_iota(jnp.int32, (blk, blk), 1))
                s = jnp.where(causal, s, -jnp.inf)
            s_buf[slot, e] = s
            cm_buf[slot, e] = jnp.max(s, axis=0, keepdims=True)

    def soft(w, slot):
        for e in range(2):
            i = steps_ref[0, 2 * w + e]
            m_old = m_ref[i]
            m_new = jnp.maximum(m_old, cm_buf[slot, e])
            m_ref[i] = m_new
            al_buf[slot, e] = jnp.exp2(m_old - m_new)
            p_buf[slot, e] = jnp.exp2(s_buf[slot, e] - m_new).astype(BF16)

    def pv(w, slot):
        for e in range(2):
            u = 2 * w + e
            i, j = steps_ref[0, u], steps_ref[1, u]
            v = jnp.concatenate([vt_ref[0, :, pl.ds(off(j), blk)], ones], axis=0)
            acc_ref[i] = al_buf[slot, e] * acc_ref[i] + _dot(v, p_buf[slot, e])

    def trip(w, masked=False):
        qk(w, 0, masked)
        soft(w - 1, 1)
        pv(w - 2, 0)
        qk(w + 1, 1, masked)
        soft(w, 0)
        pv(w - 1, 1)

    assert n_diag_couples % 2 == 0 and (n_couples - n_diag_couples) % 2 == 0
    qk(0, 0, masked=True)
    qk(1, 1, masked=True)
    soft(0, 0)
    for w in range(2, n_diag_couples, 2):
        trip(w, masked=True)

    def body(it, carry):
        trip(n_diag_couples + 2 * it)
        return carry

    lax.fori_loop(0, (n_couples - n_diag_couples) // 2, body, 0)
    soft(n_couples - 1, 1)
    pv(n_couples - 2, 0)
    pv(n_couples - 1, 1)

    for i in range(nq):
        acc = acc_ref[i]
        o = acc[:V_DIM] * (1.0 / acc[V_DIM:V_DIM + 1])
        o_ref[0, i * blk:(i + 1) * blk, :] = o.T.astype(BF16)


def _attention(qt, kn, kr, vt):
    b = qt.shape[0]
    s = qt.shape[2]
    blk = ATTN_BLOCK
    nq = s // blk
    steps = _attn_steps(nq)
    assert len(steps) % 4 == 0 and len(steps) >= 8
    steps = jnp.asarray(list(zip(*steps)), dtype=jnp.int32)
    grid_spec = pltpu.PrefetchScalarGridSpec(
        num_scalar_prefetch=1,
        grid=(b, N_HEADS),
        in_specs=[
            pl.BlockSpec((1, QK_DIM, s), lambda bi, h, st: (bi, h, 0)),
            pl.BlockSpec((1, s, QK_NOPE_DIM), lambda bi, h, st: (bi, 0, h)),
            pl.BlockSpec((1, s, QK_ROPE_DIM), lambda bi, h, st: (bi, 0, 0)),
            pl.BlockSpec((1, V_DIM, s), lambda bi, h, st: (bi, h, 0)),
        ],
        out_specs=pl.BlockSpec((1, s, V_DIM), lambda bi, h, st: (bi, 0, h)),
        scratch_shapes=[
            pltpu.VMEM((2, 2, blk, blk), F32),
            pltpu.VMEM((2, 2, 1, blk), F32),
            pltpu.VMEM((2, 2, blk, blk), BF16),
            pltpu.VMEM((2, 2, 1, blk), F32),
            pltpu.VMEM((nq, 1, blk), F32),
            pltpu.VMEM((nq, V_DIM + BF16_SUBLANES, blk), F32),
        ],
    )
    return pl.pallas_call(
        _attn_kernel,
        grid_spec=grid_spec,
        out_shape=jax.ShapeDtypeStruct((b, s, N_HEADS * V_DIM), BF16),
        compiler_params=pltpu.CompilerParams(
            dimension_semantics=("arbitrary", "arbitrary"),
            vmem_limit_bytes=V7X_VMEM_LIMIT_BYTES),
        name="mla_causal_attention",
    )(steps, qt, kn, kr, vt)


def _out_kernel(x_ref, o_ref, ga_ref, gb_ref, wap_ref, wout_ref, gmlp_ref, w1_ref, w2_ref,
                gfin_ref, out_ref):
    bm = _dot(o_ref[...], wap_ref[...])
    y = ga_ref[...].astype(F32) + gb_ref[...].astype(F32) * bm
    x1 = x_ref[...] + _dot(y.astype(BF16), wout_ref[...])
    hm = (x1 * _rms_scale(x1) * gmlp_ref[...]).astype(BF16)
    x2 = x1
    for c in range(D_FF // FF_CHUNK):
        c0 = c * FF_CHUNK
        hid = jnp.square(jnp.maximum(_dot(hm, w1_ref[:, c0:c0 + FF_CHUNK]), 0.0))
        x2 = x2 + _dot(hid.astype(BF16), w2_ref[c0:c0 + FF_CHUNK, :])
    out_ref[...] = x2 * _rms_scale(x2) * gfin_ref[...]


def _output(x2d, o2d, ga2d, gb2d, wap, wout, gmlp, w1, w2, gfin):
    t, d = x2d.shape
    tm = MLP_TOKENS
    row = lambda i: (i, 0)
    return pl.pallas_call(
        _out_kernel,
        grid=(t // tm,),
        in_specs=[
            pl.BlockSpec((tm, d), row), pl.BlockSpec((tm, d), row),
            pl.BlockSpec((tm, d), row), pl.BlockSpec((tm, d), row),
            _const_spec(wap.shape), _const_spec(wout.shape), _const_spec(gmlp.shape),
            _const_spec(w1.shape), _const_spec(w2.shape), _const_spec(gfin.shape),
        ],
        out_specs=pl.BlockSpec((tm, d), row),
        out_shape=jax.ShapeDtypeStruct((t, d), F32),
        compiler_params=pltpu.CompilerParams(
            dimension_semantics=("arbitrary",),
            vmem_limit_bytes=V7X_VMEM_LIMIT_BYTES),
        name="gated_merge_mlp",
    )(x2d, o2d, ga2d, gb2d, wap, wout, gmlp, w1, w2, gfin)


def kernel(x, positions, g_mix, w_in, b_gate, pool_w, pool_scale, w_pool_proj, g_q, w_uq, g_kv,
           w_ukv, w_attn_proj, w_out, g_mlp, w_mlp_in, w_mlp_out, g_final):
    b, s, d = x.shape
    assert d == D_MODEL and s % PROJ_TOKENS == 0 and s % ATTN_BLOCK == 0
    assert (b * s) % MLP_TOKENS == 0 and g_mix.shape[0] == 1
    l = 0

    front = POOL_WIDTH + Q_LORA + KV_LORA + QK_ROPE_DIM
    wa = jnp.pad(w_in[l][:, :front], ((0, 0), (0, D_MODEL - front))).astype(BF16)
    wg = w_in[l][:, front:].astype(BF16)
    bg = b_gate[l].reshape(1, 2 * D_MODEL)
    wuqt = w_uq[l].T.astype(BF16)
    wukv = w_ukv[l].reshape(KV_LORA, N_HEADS, QK_NOPE_DIM + V_DIM)
    wuk = wukv[:, :, :QK_NOPE_DIM].reshape(KV_LORA, N_HEADS * QK_NOPE_DIM).astype(BF16)
    wuvt = wukv[:, :, QK_NOPE_DIM:].reshape(KV_LORA, N_HEADS * V_DIM).T.astype(BF16)
    half = ROPE_HALF
    invf = (ROPE_THETA ** (-jnp.arange(half, dtype=F32) / half)).reshape(half, 1)

    ga, gb, qt, kn, kr, vt = _projections(
        x, positions.reshape(b, 1, s), invf, g_mix[l].reshape(1, d), wa, wg, bg,
        pool_w[l].astype(BF16), pool_scale[l].reshape(1, POOL_WIDTH),
        w_pool_proj[l].astype(BF16), g_q[l].reshape(1, Q_LORA), wuqt,
        g_kv[l].reshape(1, KV_LORA), wuk, wuvt)

    o = _attention(qt, kn, kr, vt)

    t = b * s
    out = _output(
        x.reshape(t, d), o.reshape(t, d), ga.reshape(t, d), gb.reshape(t, d),
        w_attn_proj[l].astype(BF16), w_out[l].astype(BF16), g_mlp[l].reshape(1, d),
        w_mlp_in[l].astype(BF16), w_mlp_out[l].astype(BF16), g_final.reshape(1, d))
    return out.reshape(b, s, d)
```

```python
import functools
import math

import jax
import jax.numpy as jnp
from jax import lax
from jax.experimental import pallas as pl
from jax.experimental.pallas import tpu as pltpu

D_MODEL = 1024
POOL_GROUPS = 4
POOL_WINDOWS = (2, 4, 8, 16)
POOL_WIDTH = D_MODEL // 2
POOL_GROUP_DIM = POOL_WIDTH // POOL_GROUPS
N_HEADS = 8
QK_NOPE_DIM = 128
QK_ROPE_DIM = 64
ROPE_HALF = QK_ROPE_DIM // 2
V_DIM = 128
Q_LORA = D_MODEL // 4
KV_LORA = D_MODEL // 8
QK_DIM = QK_NOPE_DIM + QK_ROPE_DIM
ROPE_THETA = 10000.0
D_FF = 4 * D_MODEL
EPS = 1e-6

POOL_HALO = 16
BF16_SUBLANES = 16
V7X_VMEM_LIMIT_BYTES = 56 * 1024 * 1024

PROJ_TOKENS = 1024
PROJ_SPLIT = 4
ATTN_BLOCK = 512
MLP_TOKENS = 512
FF_CHUNK = 1024

BF16 = jnp.bfloat16
F32 = jnp.float32


def _dot(a, b):
    return jnp.dot(a, b, preferred_element_type=F32)


def _dot_nt(a, b):
    return lax.dot_general(a, b, (((1,), (1,)), ((), ())), preferred_element_type=F32)


def _rms_scale(x):
    return lax.rsqrt(jnp.mean(x * x, axis=-1, keepdims=True) + EPS)


def _const_spec(shape):
    zeros = (0,) * len(shape)
    return pl.BlockSpec(shape, lambda *_: zeros, pipeline_mode=pl.Buffered(1))


def _proj_kernel(x_ref, pos_ref, invf_ref, gmix_ref, wa_ref, wg_ref, bg_ref, poolw_ref,
                 pscale_ref, wpp_ref, gq_ref, wuqt_ref, gkv_ref, wuk_ref, wuvt_ref,
                 ga_ref, gb_ref, qt_ref, kn_ref, kr_ref, vt_ref, ubuf_ref):
    tm = PROJ_TOKENS
    rows = tm // PROJ_SPLIT
    j = pl.program_id(1)

    @pl.when(j == 0)
    def _():
        ubuf_ref[0:POOL_HALO, :] = jnp.zeros((POOL_HALO, POOL_WIDTH), F32)

    for part in range(PROJ_SPLIT):
        r0 = part * rows
        x = x_ref[0, r0:r0 + rows, :]
        h = (x * _rms_scale(x) * gmix_ref[...]).astype(BF16)
        za = _dot(h, wa_ref[...])
        zg = _dot(h, wg_ref[...]) + bg_ref[...]

        u = za[:, :POOL_WIDTH]
        u0 = POOL_HALO + r0
        ubuf_ref[u0:u0 + rows, :] = u
        t_in_seq = j * tm + r0 + lax.broadcasted_iota(jnp.int32, (rows, POOL_GROUP_DIM), 0)
        mixed = []
        for g, w in enumerate(POOL_WINDOWS):
            c0 = g * POOL_GROUP_DIM
            ug = u[:, c0:c0 + POOL_GROUP_DIM]
            win = ug
            for k in range(1, w):
                win = win + ubuf_ref[u0 - k:u0 - k + rows, c0:c0 + POOL_GROUP_DIM]
            count = jnp.minimum(t_in_seq + 1, w).astype(F32)
            pooled = (win / count - ug).astype(BF16)
            mixed.append(_dot(pooled, poolw_ref[g]))
        mixed = jnp.concatenate(mixed, axis=-1) * pscale_ref[...]
        a = _dot(mixed.astype(BF16), wpp_ref[...])

        ga_ref[0, r0:r0 + rows, :] = (jax.nn.sigmoid(zg[:, :D_MODEL]) * a).astype(BF16)
        gb_ref[0, r0:r0 + rows, :] = jax.nn.sigmoid(zg[:, D_MODEL:]).astype(BF16)

        ang_t = invf_ref[...] * pos_ref[0, :, r0:r0 + rows].astype(F32)
        cos_t = jnp.cos(ang_t)
        sin_t = jnp.sin(ang_t)
        cos = cos_t.T
        sin = sin_t.T

        c_q = za[:, POOL_WIDTH:POOL_WIDTH + Q_LORA]
        q_scale = math.log2(math.e) / math.sqrt(QK_DIM)
        cqn = (c_q * _rms_scale(c_q) * (gq_ref[...] * q_scale)).astype(BF16)
        qt = _dot_nt(wuqt_ref[...], cqn)
        for hd in range(N_HEADS):
            q0 = hd * QK_DIM
            q1 = q0 + QK_NOPE_DIM
            qt_ref[0, q0:q1, r0:r0 + rows] = qt[q0:q1].astype(BF16)
            x1 = qt[q1:q1 + ROPE_HALF]
            x2 = qt[q1 + ROPE_HALF:q1 + QK_ROPE_DIM]
            qt_ref[0, q1:q1 + ROPE_HALF, r0:r0 + rows] = (x1 * cos_t - x2 * sin_t).astype(BF16)
            qt_ref[0, q1 + ROPE_HALF:q1 + QK_ROPE_DIM, r0:r0 + rows] = (
                x2 * cos_t + x1 * sin_t).astype(BF16)

        kv0 = POOL_WIDTH + Q_LORA
        c_kv = za[:, kv0:kv0 + KV_LORA]
        ckvn = (c_kv * _rms_scale(c_kv) * gkv_ref[...]).astype(BF16)
        kn_ref[0, r0:r0 + rows, :] = _dot(ckvn, wuk_ref[...]).astype(BF16)
        vt_ref[0, :, r0:r0 + rows] = _dot_nt(wuvt_ref[...], ckvn).astype(BF16)

        kr0 = kv0 + KV_LORA
        k1 = za[:, kr0:kr0 + ROPE_HALF]
        k2 = za[:, kr0 + ROPE_HALF:kr0 + QK_ROPE_DIM]
        kr_ref[0, r0:r0 + rows, :] = jnp.concatenate(
            [k1 * cos - k2 * sin, k2 * cos + k1 * sin], axis=-1).astype(BF16)

    ubuf_ref[0:POOL_HALO, :] = ubuf_ref[tm:tm + POOL_HALO, :]


def _projections(x, pos_row, invf, gmix, wa, wg, bg, poolw, pscale, wpp, gq, wuqt, gkv, wuk, wuvt):
    b, s, d = x.shape
    tm = PROJ_TOKENS
    nt = s // tm
    tok = lambda bi, j: (bi, j, 0)
    feat = lambda bi, j: (bi, 0, j)
    out_shape = (
        jax.ShapeDtypeStruct((b, s, d), BF16),
        jax.ShapeDtypeStruct((b, s, d), BF16),
        jax.ShapeDtypeStruct((b, N_HEADS * QK_DIM, s), BF16),
        jax.ShapeDtypeStruct((b, s, N_HEADS * QK_NOPE_DIM), BF16),
        jax.ShapeDtypeStruct((b, s, QK_ROPE_DIM), BF16),
        jax.ShapeDtypeStruct((b, N_HEADS * V_DIM, s), BF16),
    )
    return pl.pallas_call(
        _proj_kernel,
        grid=(b, nt),
        in_specs=[
            pl.BlockSpec((1, tm, d), tok),
            pl.BlockSpec((1, 1, tm), feat),
            _const_spec(invf.shape), _const_spec(gmix.shape), _const_spec(wa.shape),
            _const_spec(wg.shape), _const_spec(bg.shape), _const_spec(poolw.shape),
            _const_spec(pscale.shape), _const_spec(wpp.shape), _const_spec(gq.shape),
            _const_spec(wuqt.shape), _const_spec(gkv.shape), _const_spec(wuk.shape),
            _const_spec(wuvt.shape),
        ],
        out_specs=(
            pl.BlockSpec((1, tm, d), tok),
            pl.BlockSpec((1, tm, d), tok),
            pl.BlockSpec((1, N_HEADS * QK_DIM, tm), feat),
            pl.BlockSpec((1, tm, N_HEADS * QK_NOPE_DIM), tok),
            pl.BlockSpec((1, tm, QK_ROPE_DIM), tok),
            pl.BlockSpec((1, N_HEADS * V_DIM, tm), feat),
        ),
        out_shape=out_shape,
        scratch_shapes=[pltpu.VMEM((POOL_HALO + tm, POOL_WIDTH), F32)],
        compiler_params=pltpu.CompilerParams(
            dimension_semantics=("arbitrary", "arbitrary"),
            vmem_limit_bytes=V7X_VMEM_LIMIT_BYTES),
        name="mla_pool_projections",
    )(x, pos_row, invf, gmix, wa, wg, bg, poolw, pscale, wpp, gq, wuqt, gkv, wuk, wuvt)


def _attn_steps(nq):
    return [(i, i) for i in range(nq)] + [(i, j) for j in range(nq) for i in range(j + 1, nq)]


def _attn_kernel(steps_ref, qt_ref, kn_ref, kr_ref, vt_ref, o_ref,
                 s_buf, cm_buf, p_buf, al_buf, m_ref, acc_ref):
    blk = ATTN_BLOCK
    nq = qt_ref.shape[2] // blk
    n_couples = nq * (nq + 1) // 4
    n_diag_couples = nq // 2
    ones = jnp.ones((BF16_SUBLANES, blk), BF16)

    m_ref[...] = jnp.full(m_ref.shape, -jnp.inf, F32)
    acc_ref[...] = jnp.zeros(acc_ref.shape, F32)

    def off(b):
        return pl.multiple_of(b * blk, blk)

    def qk(w, slot, masked=False):
        for e in range(2):
            u = 2 * w + e
            i, j = steps_ref[0, u], steps_ref[1, u]
            k = jnp.concatenate(
                [kn_ref[0, pl.ds(off(j), blk), :], kr_ref[0, pl.ds(off(j), blk), :]], axis=-1)
            s = _dot(k, qt_ref[0, :, pl.ds(off(i), blk)])
            if masked:
                causal = (lax.broadcasted_iota(jnp.int32, (blk, blk), 0)
                          <= lax.broadcasted_iota(jnp.int32, (blk, blk), 1))
                s = jnp.where(causal, s, -jnp.inf)
            s_buf[slot, e] = s
            cm_buf[slot, e] = jnp.max(s, axis=0, keepdims=True)

    def soft(w, slot):
        for e in range(2):
            i = steps_ref[0, 2 * w + e]
            m_old = m_ref[i]
            m_new = jnp.maximum(m_old, cm_buf[slot, e])
            m_ref[i] = m_new
            al_buf[slot, e] = jnp.exp2(m_old - m_new)
            p_buf[slot, e] = jnp.exp2(s_buf[slot, e] - m_new).astype(BF16)

    def pv(w, slot):
        for e in range(2):
            u = 2 * w + e
            i, j = steps_ref[0, u], steps_ref[1, u]
            v = jnp.concatenate([vt_ref[0, :, pl.ds(off(j), blk)], ones], axis=0)
            acc_ref[i] = al_buf[slot, e] * acc_ref[i] + _dot(v, p_buf[slot, e])

    def trip(w, masked=False):
        qk(w, 0, masked)
        soft(w - 1, 1)
        pv(w - 2, 0)
        qk(w + 1, 1, masked)
        soft(w, 0)
        pv(w - 1, 1)

    assert n_diag_couples % 2 == 0 and (n_couples - n_diag_couples) % 2 == 0
    qk(0, 0, masked=True)
    qk(1, 1, masked=True)
    soft(0, 0)
    for w in range(2, n_diag_couples, 2):
        trip(w, masked=True)

    def body(it, carry):
        trip(n_diag_couples + 2 * it)
        return carry

    lax.fori_loop(0, (n_couples - n_diag_couples) // 2, body, 0)
    soft(n_couples - 1, 1)
    pv(n_couples - 2, 0)
    pv(n_couples - 1, 1)

    for i in range(nq):
        acc = acc_ref[i]
        o = acc[:V_DIM] * (1.0 / acc[V_DIM:V_DIM + 1])
        o_ref[0, i * blk:(i + 1) * blk, :] = o.T.astype(BF16)


def _attention(qt, kn, kr, vt):
    b = qt.shape[0]
    s = qt.shape[2]
    blk = ATTN_BLOCK
    nq = s // blk
    steps = _attn_steps(nq)
    assert len(steps) % 4 == 0 and len(steps) >= 8
    steps = jnp.asarray(list(zip(*steps)), dtype=jnp.int32)
    grid_spec = pltpu.PrefetchScalarGridSpec(
        num_scalar_prefetch=1,
        grid=(b, N_HEADS),
        in_specs=[
            pl.BlockSpec((1, QK_DIM, s), lambda bi, h, st: (bi, h, 0)),
            pl.BlockSpec((1, s, QK_NOPE_DIM), lambda bi, h, st: (bi, 0, h)),
            pl.BlockSpec((1, s, QK_ROPE_DIM), lambda bi, h, st: (bi, 0, 0)),
            pl.BlockSpec((1, V_DIM, s), lambda bi, h, st: (bi, h, 0)),
        ],
        out_specs=pl.BlockSpec((1, s, V_DIM), lambda bi, h, st: (bi, 0, h)),
        scratch_shapes=[
            pltpu.VMEM((2, 2, blk, blk), F32),
            pltpu.VMEM((2, 2, 1, blk), F32),
            pltpu.VMEM((2, 2, blk, blk), BF16),
            pltpu.VMEM((2, 2, 1, blk), F32),
            pltpu.VMEM((nq, 1, blk), F32),
            pltpu.VMEM((nq, V_DIM + BF16_SUBLANES, blk), F32),
        ],
    )
    return pl.pallas_call(
        _attn_kernel,
        grid_spec=grid_spec,
        out_shape=jax.ShapeDtypeStruct((b, s, N_HEADS * V_DIM), BF16),
        compiler_params=pltpu.CompilerParams(
            dimension_semantics=("arbitrary", "arbitrary"),
            vmem_limit_bytes=V7X_VMEM_LIMIT_BYTES),
        name="mla_causal_attention",
    )(steps, qt, kn, kr, vt)


def _out_kernel(x_ref, o_ref, ga_ref, gb_ref, wap_ref, wout_ref, gmlp_ref, w1_ref, w2_ref,
                gfin_ref, out_ref):
    bm = _dot(o_ref[...], wap_ref[...])
    y = ga_ref[...].astype(F32) + gb_ref[...].astype(F32) * bm
    x1 = x_ref[...] + _dot(y.astype(BF16), wout_ref[...])
    hm = (x1 * _rms_scale(x1) * gmlp_ref[...]).astype(BF16)
    x2 = x1
    for c in range(D_FF // FF_CHUNK):
        c0 = c * FF_CHUNK
        hid = jnp.square(jnp.maximum(_dot(hm, w1_ref[:, c0:c0 + FF_CHUNK]), 0.0))
        x2 = x2 + _dot(hid.astype(BF16), w2_ref[c0:c0 + FF_CHUNK, :])
    out_ref[...] = x2 * _rms_scale(x2) * gfin_ref[...]


def _output(x2d, o2d, ga2d, gb2d, wap, wout, gmlp, w1, w2, gfin):
    t, d = x2d.shape
    tm = MLP_TOKENS
    row = lambda i: (i, 0)
    return pl.pallas_call(
        _out_kernel,
        grid=(t // tm,),
        in_specs=[
            pl.BlockSpec((tm, d), row), pl.BlockSpec((tm, d), row),
            pl.BlockSpec((tm, d), row), pl.BlockSpec((tm, d), row),
            _const_spec(wap.shape), _const_spec(wout.shape), _const_spec(gmlp.shape),
            _const_spec(w1.shape), _const_spec(w2.shape), _const_spec(gfin.shape),
        ],
        out_specs=pl.BlockSpec((tm, d), row),
        out_shape=jax.ShapeDtypeStruct((t, d), F32),
        compiler_params=pltpu.CompilerParams(
            dimension_semantics=("arbitrary",),
            vmem_limit_bytes=V7X_VMEM_LIMIT_BYTES),
        name="gated_merge_mlp",
    )(x2d, o2d, ga2d, gb2d, wap, wout, gmlp, w1, w2, gfin)


def kernel(x, positions, g_mix, w_in, b_gate, pool_w, pool_scale, w_pool_proj, g_q, w_uq, g_kv,
           w_ukv, w_attn_proj, w_out, g_mlp, w_mlp_in, w_mlp_out, g_final):
    b, s, d = x.shape
    assert d == D_MODEL and s % PROJ_TOKENS == 0 and s % ATTN_BLOCK == 0
    assert (b * s) % MLP_TOKENS == 0 and g_mix.shape[0] == 1
    l = 0

    front = POOL_WIDTH + Q_LORA + KV_LORA + QK_ROPE_DIM
    wa = jnp.pad(w_in[l][:, :front], ((0, 0), (0, D_MODEL - front))).astype(BF16)
    wg = w_in[l][:, front:].astype(BF16)
    bg = b_gate[l].reshape(1, 2 * D_MODEL)
    wuqt = w_uq[l].T.astype(BF16)
    wukv = w_ukv[l].reshape(KV_LORA, N_HEADS, QK_NOPE_DIM + V_DIM)
    wuk = wukv[:, :, :QK_NOPE_DIM].reshape(KV_LORA, N_HEADS * QK_NOPE_DIM).astype(BF16)
    wuvt = wukv[:, :, QK_NOPE_DIM:].reshape(KV_LORA, N_HEADS * V_DIM).T.astype(BF16)
    half = ROPE_HALF
    invf = (ROPE_THETA ** (-jnp.arange(half, dtype=F32) / half)).reshape(half, 1)

    ga, gb, qt, kn, kr, vt = _projections(
        x, positions.reshape(b, 1, s), invf, g_mix[l].reshape(1, d), wa, wg, bg,
        pool_w[l].astype(BF16), pool_scale[l].reshape(1, POOL_WIDTH),
        w_pool_proj[l].astype(BF16), g_q[l].reshape(1, Q_LORA), wuqt,
        g_kv[l].reshape(1, KV_LORA), wuk, wuvt)

    o = _attention(qt, kn, kr, vt)

    t = b * s
    out = _output(
        x.reshape(t, d), o.reshape(t, d), ga.reshape(t, d), gb.reshape(t, d),
        w_attn_proj[l].astype(BF16), w_out[l].astype(BF16), g_mlp[l].reshape(1, d),
        w_mlp_in[l].astype(BF16), w_mlp_out[l].astype(BF16), g_final.reshape(1, d))
    return out.reshape(b, s, d)
```

```python
import math

import jax
import jax.numpy as jnp
from jax import lax
from jax.experimental import pallas as pl
from jax.experimental.pallas import tpu as pltpu

D_MODEL = 1024
POOL_GROUPS = 4
POOL_WINDOWS = (2, 4, 8, 16)
POOL_WIDTH = D_MODEL // 2
POOL_GROUP_DIM = POOL_WIDTH // POOL_GROUPS
N_HEADS = 8
QK_NOPE_DIM = 128
QK_ROPE_DIM = 64
ROPE_HALF = QK_ROPE_DIM // 2
V_DIM = 128
Q_LORA = D_MODEL // 4
KV_LORA = D_MODEL // 8
QK_DIM = QK_NOPE_DIM + QK_ROPE_DIM
ROPE_THETA = 10000.0
D_FF = 4 * D_MODEL
EPS = 1e-6

POOL_HALO = 16
BF16_SUBLANES = 16
V7X_VMEM_LIMIT_BYTES = 56 * 1024 * 1024

PROJ_TOKENS = 1024
PROJ_SPLIT = 4
ATTN_BLOCK = 512
MLP_TOKENS = 512
FF_CHUNK = 1024

BF16 = jnp.bfloat16
F32 = jnp.float32


def _dot(a, b):
    return jnp.dot(a, b, preferred_element_type=F32)


def _dot_nt(a, b):
    return lax.dot_general(a, b, (((1,), (1,)), ((), ())), preferred_element_type=F32)


def _rms_scale(x):
    return lax.rsqrt(jnp.mean(x * x, axis=-1, keepdims=True) + EPS)


def _const_spec(shape):
    zeros = (0,) * len(shape)
    return pl.BlockSpec(shape, lambda *_: zeros, pipeline_mode=pl.Buffered(1))


def _proj_kernel(x_ref, pos_ref, invf_ref, gmix_ref, wa_ref, wg_ref, bg_ref, poolw_ref,
                 pscale_ref, wpp_ref, gq_ref, wuqt_ref, gkv_ref, wuk_ref, wuvt_ref,
                 ga_ref, gb_ref, qt_ref, kn_ref, kr_ref, vt_ref, ubuf_ref):
    tm = PROJ_TOKENS
    rows = tm // PROJ_SPLIT
    j = pl.program_id(1)

    @pl.when(j == 0)
    def _():
        ubuf_ref[0:POOL_HALO, :] = jnp.zeros((POOL_HALO, POOL_WIDTH), F32)

    for part in range(PROJ_SPLIT):
        r0 = part * rows
        x = x_ref[0, r0:r0 + rows, :]
        h = (x * _rms_scale(x) * gmix_ref[...]).astype(BF16)
        za = _dot(h, wa_ref[...])
        zg = _dot(h, wg_ref[...]) + bg_ref[...]

        u = za[:, :POOL_WIDTH]
        u0 = POOL_HALO + r0
        ubuf_ref[u0:u0 + rows, :] = u
        t_in_seq = j * tm + r0 + lax.broadcasted_iota(jnp.int32, (rows, POOL_GROUP_DIM), 0)
        mixed = []
        for g, w in enumerate(POOL_WINDOWS):
            c0 = g * POOL_GROUP_DIM
            ug = u[:, c0:c0 + POOL_GROUP_DIM]
            win = ug
            for k in range(1, w):
                win = win + ubuf_ref[u0 - k:u0 - k + rows, c0:c0 + POOL_GROUP_DIM]
            count = jnp.minimum(t_in_seq + 1, w).astype(F32)
            pooled = (win / count - ug).astype(BF16)
            mixed.append(_dot(pooled, poolw_ref[g]))
        mixed = jnp.concatenate(mixed, axis=-1) * pscale_ref[...]
        a = _dot(mixed.astype(BF16), wpp_ref[...])

        ga_ref[0, r0:r0 + rows, :] = (jax.nn.sigmoid(zg[:, :D_MODEL]) * a).astype(BF16)
        gb_ref[0, r0:r0 + rows, :] = jax.nn.sigmoid(zg[:, D_MODEL:]).astype(BF16)

        ang_t = invf_ref[...] * pos_ref[0, :, r0:r0 + rows].astype(F32)
        cos_t = jnp.cos(ang_t)
        sin_t = jnp.sin(ang_t)
        cos = cos_t.T
        sin = sin_t.T

        c_q = za[:, POOL_WIDTH:POOL_WIDTH + Q_LORA]
        q_scale = math.log2(math.e) / math.sqrt(QK_DIM)
        cqn = (c_q * _rms_scale(c_q) * (gq_ref[...] * q_scale)).astype(BF16)
        qt = _dot_nt(wuqt_ref[...], cqn)
        for hd in range(N_HEADS):
            q0 = hd * QK_DIM
            q1 = q0 + QK_NOPE_DIM
            qt_ref[0, q0:q1, r0:r0 + rows] = qt[q0:q1].astype(BF16)
            x1 = qt[q1:q1 + ROPE_HALF]
            x2 = qt[q1 + ROPE_HALF:q1 + QK_ROPE_DIM]
            qt_ref[0, q1:q1 + ROPE_HALF, r0:r0 + rows] = (x1 * cos_t - x2 * sin_t).astype(BF16)
            qt_ref[0, q1 + ROPE_HALF:q1 + QK_ROPE_DIM, r0:r0 + rows] = (
                x2 * cos_t + x1 * sin_t).astype(BF16)

        kv0 = POOL_WIDTH + Q_LORA
        c_kv = za[:, kv0:kv0 + KV_LORA]
        ckvn = (c_kv * _rms_scale(c_kv) * gkv_ref[...]).astype(BF16)
        kn_ref[0, r0:r0 + rows, :] = _dot(ckvn, wuk_ref[...]).astype(BF16)
        vt_ref[0, :, r0:r0 + rows] = _dot_nt(wuvt_ref[...], ckvn).astype(BF16)

        kr0 = kv0 + KV_LORA
        k1 = za[:, kr0:kr0 + ROPE_HALF]
        k2 = za[:, kr0 + ROPE_HALF:kr0 + QK_ROPE_DIM]
        kr_ref[0, r0:r0 + rows, :] = jnp.concatenate(
            [k1 * cos - k2 * sin, k2 * cos + k1 * sin], axis=-1).astype(BF16)

    ubuf_ref[0:POOL_HALO, :] = ubuf_ref[tm:tm + POOL_HALO, :]


def _projections(x, pos_row, invf, gmix, wa, wg, bg, poolw, pscale, wpp, gq, wuqt, gkv, wuk, wuvt):
    b, s, d = x.shape
    tm = PROJ_TOKENS
    nt = s // tm
    tok = lambda bi, j: (bi, j, 0)
    feat = lambda bi, j: (bi, 0, j)
    out_shape = (
        jax.ShapeDtypeStruct((b, s, d), BF16),
        jax.ShapeDtypeStruct((b, s, d), BF16),
        jax.ShapeDtypeStruct((b, N_HEADS * QK_DIM, s), BF16),
        jax.ShapeDtypeStruct((b, s, N_HEADS * QK_NOPE_DIM), BF16),
        jax.ShapeDtypeStruct((b, s, QK_ROPE_DIM), BF16),
        jax.ShapeDtypeStruct((b, N_HEADS * V_DIM, s), BF16),
    )
    return pl.pallas_call(
        _proj_kernel,
        grid=(b, nt),
        in_specs=[
            pl.BlockSpec((1, tm, d), tok),
            pl.BlockSpec((1, 1, tm), feat),
            _const_spec(invf.shape), _const_spec(gmix.shape), _const_spec(wa.shape),
            _const_spec(wg.shape), _const_spec(bg.shape), _const_spec(poolw.shape),
            _const_spec(pscale.shape), _const_spec(wpp.shape), _const_spec(gq.shape),
            _const_spec(wuqt.shape), _const_spec(gkv.shape), _const_spec(wuk.shape),
            _const_spec(wuvt.shape),
        ],
        out_specs=(
            pl.BlockSpec((1, tm, d), tok),
            pl.BlockSpec((1, tm, d), tok),
            pl.BlockSpec((1, N_HEADS * QK_DIM, tm), feat),
            pl.BlockSpec((1, tm, N_HEADS * QK_NOPE_DIM), tok),
            pl.BlockSpec((1, tm, QK_ROPE_DIM), tok),
            pl.BlockSpec((1, N_HEADS * V_DIM, tm), feat),
        ),
        out_shape=out_shape,
        scratch_shapes=[pltpu.VMEM((POOL_HALO + tm, POOL_WIDTH), F32)],
        compiler_params=pltpu.CompilerParams(
            dimension_semantics=("arbitrary", "arbitrary"),
            vmem_limit_bytes=V7X_VMEM_LIMIT_BYTES),
        name="mla_pool_projections",
    )(x, pos_row, invf, gmix, wa, wg, bg, poolw, pscale, wpp, gq, wuqt, gkv, wuk, wuvt)


def _attn_steps(nq):
    return [(i, i) for i in range(nq)] + [(i, j) for j in range(nq) for i in range(j + 1, nq)]


def _attn_kernel(steps_ref, qt_ref, kn_ref, kr_ref, vt_ref, o_ref,
                 s_buf, cm_buf, p_buf, al_buf, m_ref, acc_ref):
    blk = ATTN_BLOCK
    half = blk // 2
    nq = qt_ref.shape[2] // blk
    n_couples = nq * (nq + 1) // 4
    n_diag_couples = nq // 2
    ones = jnp.ones((BF16_SUBLANES, blk), BF16)

    def off(b):
        return pl.multiple_of(b * blk, blk)

    def keys(k0, n):
        return jnp.concatenate([kn_ref[0, pl.ds(k0, n), :], kr_ref[0, pl.ds(k0, n), :]], axis=-1)

    def qk(w, slot):
        for e in range(2):
            u = 2 * w + e
            i, j = steps_ref[0, u], steps_ref[1, u]
            s = _dot(keys(off(j), blk), qt_ref[0, :, pl.ds(off(i), blk)])
            s_buf[slot, e] = s
            cm_buf[slot, e] = jnp.max(s, axis=0, keepdims=True)

    def soft(w, slot):
        for e in range(2):
            i = steps_ref[0, 2 * w + e]
            m_old = m_ref[i]
            m_new = jnp.maximum(m_old, cm_buf[slot, e])
            m_ref[i] = m_new
            al_buf[slot, e] = jnp.exp2(m_old - m_new)
            p_buf[slot, e] = jnp.exp2(s_buf[slot, e] - m_new).astype(BF16)

    def pv(w, slot):
        for e in range(2):
            u = 2 * w + e
            i, j = steps_ref[0, u], steps_ref[1, u]
            v = jnp.concatenate([vt_ref[0, :, pl.ds(off(j), blk)], ones], axis=0)
            acc_ref[i] = al_buf[slot, e] * acc_ref[i] + _dot(v, p_buf[slot, e])

    tri = (lax.broadcasted_iota(jnp.int32, (half, half), 0)
           <= lax.broadcasted_iota(jnp.int32, (half, half), 1))

    def qk_diag(w, slot):
        for e in range(2):
            q0 = (2 * w + e) * blk
            top = _dot(keys(q0, half), qt_ref[0, :, q0:q0 + blk])
            bot = _dot(keys(q0 + half, half), qt_ref[0, :, q0 + half:q0 + blk])
            top_l = jnp.where(tri, top[:, :half], -jnp.inf)
            bot = jnp.where(tri, bot, -jnp.inf)
            s_buf[slot, e, 0:half, 0:half] = top_l
            s_buf[slot, e, 0:half, half:blk] = top[:, half:]
            s_buf[slot, e, half:blk, half:blk] = bot
            cm_buf[slot, e, :, 0:half] = jnp.max(top_l, axis=0, keepdims=True)
            cm_buf[slot, e, :, half:blk] = jnp.maximum(
                jnp.max(top[:, half:], axis=0, keepdims=True), jnp.max(bot, axis=0, keepdims=True))

    def soft_diag(w, slot):
        for e in range(2):
            m = cm_buf[slot, e]
            m_ref[2 * w + e] = m
            p_buf[slot, e, 0:half, :] = jnp.exp2(s_buf[slot, e, 0:half, :] - m).astype(BF16)
            p_buf[slot, e, half:blk, half:blk] = jnp.exp2(
                s_buf[slot, e, half:blk, half:blk] - m[:, half:]).astype(BF16)

    def pv_diag(w, slot):
        for e in range(2):
            i = 2 * w + e
            q0 = i * blk
            v = jnp.concatenate([vt_ref[0, :, q0:q0 + blk], ones], axis=0)
            top = _dot(v[:, :half], p_buf[slot, e, 0:half, :])
            bot = _dot(v[:, half:], p_buf[slot, e, half:blk, half:blk])
            acc_ref[i, :, 0:half] = top[:, :half]
            acc_ref[i, :, half:blk] = top[:, half:] + bot

    def trip(w):
        static = isinstance(w, int)

        def pick(c, plain, diag):
            return diag if static and c < n_diag_couples else plain

        pick(w, qk, qk_diag)(w, 0)
        pick(w - 1, soft, soft_diag)(w - 1, 1)
        pick(w - 2, pv, pv_diag)(w - 2, 0)
        pick(w + 1, qk, qk_diag)(w + 1, 1)
        pick(w, soft, soft_diag)(w, 0)
        pick(w - 1, pv, pv_diag)(w - 1, 1)

    first_loop = n_diag_couples + 2
    assert n_diag_couples % 2 == 0 and (n_couples - first_loop) % 2 == 0
    qk_diag(0, 0)
    qk_diag(1, 1)
    soft_diag(0, 0)
    for w in range(2, first_loop, 2):
        trip(w)

    def body(it, carry):
        trip(first_loop + 2 * it)
        return carry

    lax.fori_loop(0, (n_couples - first_loop) // 2, body, 0)
    soft(n_couples - 1, 1)
    pv(n_couples - 2, 0)
    pv(n_couples - 1, 1)

    for i in range(nq):
        acc = acc_ref[i]
        o = acc[:V_DIM] * (1.0 / acc[V_DIM:V_DIM + 1])
        o_ref[0, :, i * blk:(i + 1) * blk] = o.astype(BF16)


def _attention(qt, kn, kr, vt):
    b = qt.shape[0]
    s = qt.shape[2]
    blk = ATTN_BLOCK
    nq = s // blk
    steps = _attn_steps(nq)
    assert len(steps) % 4 == 0 and len(steps) >= 8
    steps = jnp.asarray(list(zip(*steps)), dtype=jnp.int32)
    grid_spec = pltpu.PrefetchScalarGridSpec(
        num_scalar_prefetch=1,
        grid=(b, N_HEADS),
        in_specs=[
            pl.BlockSpec((1, QK_DIM, s), lambda bi, h, st: (bi, h, 0)),
            pl.BlockSpec((1, s, QK_NOPE_DIM), lambda bi, h, st: (bi, 0, h)),
            pl.BlockSpec((1, s, QK_ROPE_DIM), lambda bi, h, st: (bi, 0, 0)),
            pl.BlockSpec((1, V_DIM, s), lambda bi, h, st: (bi, h, 0)),
        ],
        out_specs=pl.BlockSpec((1, V_DIM, s), lambda bi, h, st: (bi, h, 0)),
        scratch_shapes=[
            pltpu.VMEM((2, 2, blk, blk), F32),
            pltpu.VMEM((2, 2, 1, blk), F32),
            pltpu.VMEM((2, 2, blk, blk), BF16),
            pltpu.VMEM((2, 2, 1, blk), F32),
            pltpu.VMEM((nq, 1, blk), F32),
            pltpu.VMEM((nq, V_DIM + BF16_SUBLANES, blk), F32),
        ],
    )
    return pl.pallas_call(
        _attn_kernel,
        grid_spec=grid_spec,
        out_shape=jax.ShapeDtypeStruct((b, N_HEADS * V_DIM, s), BF16),
        compiler_params=pltpu.CompilerParams(
            dimension_semantics=("arbitrary", "arbitrary"),
            vmem_limit_bytes=V7X_VMEM_LIMIT_BYTES),
        name="mla_causal_attention",
    )(steps, qt, kn, kr, vt)


def _out_kernel(x_ref, ot_ref, ga_ref, gb_ref, wap_ref, wout_ref, gmlp_ref, w1_ref, w2_ref,
                gfin_ref, out_ref):
    bm = lax.dot_general(ot_ref[0], wap_ref[...], (((0,), (0,)), ((), ())),
                         preferred_element_type=F32)
    y = ga_ref[...].astype(F32) + gb_ref[...].astype(F32) * bm
    x1 = x_ref[...] + _dot(y.astype(BF16), wout_ref[...])
    hm = (x1 * _rms_scale(x1) * gmlp_ref[...]).astype(BF16)
    x2 = x1
    for c in range(D_FF // FF_CHUNK):
        c0 = c * FF_CHUNK
        hid = jnp.square(jnp.maximum(_dot(hm, w1_ref[:, c0:c0 + FF_CHUNK]), 0.0))
        x2 = x2 + _dot(hid.astype(BF16), w2_ref[c0:c0 + FF_CHUNK, :])
    out_ref[...] = x2 * _rms_scale(x2) * gfin_ref[...]


def _output(x2d, ot, ga2d, gb2d, wap, wout, gmlp, w1, w2, gfin):
    t, d = x2d.shape
    tm = MLP_TOKENS
    tiles_per_seq = ot.shape[2] // tm
    row = lambda i: (i, 0)
    feat = lambda i: (i // tiles_per_seq, 0, i % tiles_per_seq)
    return pl.pallas_call(
        _out_kernel,
        grid=(t // tm,),
        in_specs=[
            pl.BlockSpec((tm, d), row), pl.BlockSpec((1, ot.shape[1], tm), feat),
            pl.BlockSpec((tm, d), row), pl.BlockSpec((tm, d), row),
            _const_spec(wap.shape), _const_spec(wout.shape), _const_spec(gmlp.shape),
            _const_spec(w1.shape), _const_spec(w2.shape), _const_spec(gfin.shape),
        ],
        out_specs=pl.BlockSpec((tm, d), row),
        out_shape=jax.ShapeDtypeStruct((t, d), F32),
        compiler_params=pltpu.CompilerParams(
            dimension_semantics=("arbitrary",),
            vmem_limit_bytes=V7X_VMEM_LIMIT_BYTES),
        name="gated_merge_mlp",
    )(x2d, ot, ga2d, gb2d, wap, wout, gmlp, w1, w2, gfin)


def kernel(x, positions, g_mix, w_in, b_gate, pool_w, pool_scale, w_pool_proj, g_q, w_uq, g_kv,
           w_ukv, w_attn_proj, w_out, g_mlp, w_mlp_in, w_mlp_out, g_final):
    b, s, d = x.shape
    assert d == D_MODEL and s % PROJ_TOKENS == 0 and s % ATTN_BLOCK == 0
    assert s % MLP_TOKENS == 0 and g_mix.shape[0] == 1
    l = 0

    front = POOL_WIDTH + Q_LORA + KV_LORA + QK_ROPE_DIM
    wa = jnp.pad(w_in[l][:, :front], ((0, 0), (0, D_MODEL - front))).astype(BF16)
    wg = w_in[l][:, front:].astype(BF16)
    bg = b_gate[l].reshape(1, 2 * D_MODEL)
    wuqt = w_uq[l].T.astype(BF16)
    wukv = w_ukv[l].reshape(KV_LORA, N_HEADS, QK_NOPE_DIM + V_DIM)
    wuk = wukv[:, :, :QK_NOPE_DIM].reshape(KV_LORA, N_HEADS * QK_NOPE_DIM).astype(BF16)
    wuvt = wukv[:, :, QK_NOPE_DIM:].reshape(KV_LORA, N_HEADS * V_DIM).T.astype(BF16)
    half = ROPE_HALF
    invf = (ROPE_THETA ** (-jnp.arange(half, dtype=F32) / half)).reshape(half, 1)

    ga, gb, qt, kn, kr, vt = _projections(
        x, positions.reshape(b, 1, s), invf, g_mix[l].reshape(1, d), wa, wg, bg,
        pool_w[l].astype(BF16), pool_scale[l].reshape(1, POOL_WIDTH),
        w_pool_proj[l].astype(BF16), g_q[l].reshape(1, Q_LORA), wuqt,
        g_kv[l].reshape(1, KV_LORA), wuk, wuvt)

    ot = _attention(qt, kn, kr, vt)

    t = b * s
    out = _output(
        x.reshape(t, d), ot, ga.reshape(t, d), gb.reshape(t, d),
        w_attn_proj[l].astype(BF16), w_out[l].astype(BF16), g_mlp[l].reshape(1, d),
        w_mlp_in[l].astype(BF16), w_mlp_out[l].astype(BF16), g_final.reshape(1, d))
    return out.reshape(b, s, d)
```
